```python
import jax, jax.numpy as jnp
from jax import lax
import numpy as np

D_MODEL = 2048
BATCH = 4
SEQ = 4096
DEPTH = 2

CHUNK = 64
GLA_HEADS = 4
GLA_V = D_MODEL // 2
GLA_DV = GLA_V // GLA_HEADS
GLA_DK = GLA_DV // 2
GLA_QK = GLA_HEADS * GLA_DK
GLA_LOWRANK = 16
GLA_TAU = 16.0
CONV_WIDTH = D_MODEL // 2
CONV_K = 3
D_FF = 11 * D_MODEL // 4
EPS = 1e-6
SPLITS = (GLA_QK, GLA_QK, GLA_V, GLA_V, GLA_LOWRANK,
          CONV_WIDTH, CONV_WIDTH, CONV_WIDTH, D_MODEL, D_MODEL)
N_IN = sum(SPLITS)

kernel_name = "hybrid_gla_shortconv_convffn_adaln"


def rmsnorm(x, g):
    xf = x.astype(jnp.float32)
    y = xf * lax.rsqrt(jnp.mean(xf * xf, axis=-1, keepdims=True) + EPS)
    return (y * g.astype(jnp.float32)).astype(x.dtype)


def causal_dwconv(x, w):
    k = w.shape[0]
    s = x.shape[1]
    xp = jnp.pad(x, ((0, 0), (k - 1, 0), (0, 0)))
    y = w[0] * xp[:, 0:s]
    for i in range(1, k):
        y = y + w[i] * xp[:, i:i + s]
    return y


def gla_chunked(q, k, v, log_a):
    b, s, h, dk = q.shape
    dv = v.shape[-1]
    nc = s // CHUNK

    def to_chunks(t):
        return t.reshape(b, nc, CHUNK, h, t.shape[-1]).transpose(1, 0, 3, 2, 4)

    def step(state, inp):
        q_c, k_c, v_c, a_c = inp
        cum = jnp.cumsum(a_c, axis=2)
        cum_end = cum[:, :, -1:, :]
        k_dec = k_c * jnp.exp(cum_end - cum)
        state = (jnp.exp(cum_end[:, :, 0, :])[..., None] * state
                 + jnp.einsum('bhld,bhle->bhde', k_dec, v_c))
        o_c = jnp.einsum('bhld,bhde->bhle', q_c, state)
        return state, o_c

    s0 = jnp.zeros((b, h, dk, dv), jnp.float32)
    _, o = lax.scan(step, s0, (to_chunks(q), to_chunks(k), to_chunks(v), to_chunks(log_a)))
    return o.transpose(1, 0, 3, 2, 4).reshape(b, s, h, dv)


def token_mixer(h, w_in, w_a2, b_a2, gla_norm_g, w_out_gla, conv_mix_w, w_out_conv, w_o):
    bsz, s, _ = h.shape
    split_idx = [int(i) for i in np.cumsum(SPLITS)[:-1]]
    q, k, v, r, lr, cb, cc, cx, ga, gb = jnp.split(h @ w_in, split_idx, axis=-1)

    log_a = jax.nn.log_sigmoid((lr @ w_a2 + b_a2).astype(jnp.float32)) / GLA_TAU
    heads = lambda t, d: t.astype(jnp.float32).reshape(bsz, s, GLA_HEADS, d)
    o = gla_chunked(heads(q, GLA_DK) * (GLA_DK ** -0.5), heads(k, GLA_DK),
                    heads(v, GLA_DV), heads(log_a, GLA_DK))
    o = rmsnorm(o, gla_norm_g).astype(h.dtype).reshape(bsz, s, GLA_V)
    y_a = (o * jax.nn.silu(r)) @ w_out_gla

    y_b = (cb * causal_dwconv(cc * cx, conv_mix_w)) @ w_out_conv

    m = jax.nn.sigmoid(ga) * y_a + jax.nn.sigmoid(gb) * y_b
    return m @ w_o


def channel_mixer(h, w_up, ffn_conv_w, w_down):
    gate, up = jnp.split(h @ w_up, 2, axis=-1)
    return (jax.nn.gelu(causal_dwconv(gate, ffn_conv_w)) * up) @ w_down


def setup_inputs(seed: int = 0) -> dict:
    key = jax.random.key(seed)
    ks = jax.random.split(key, 16)
    nrm = lambda k, shape, scale: jax.random.normal(k, shape, jnp.float32) * scale
    return {
        "x": nrm(ks[0], (BATCH, SEQ, D_MODEL), 1.0),
        "c": nrm(ks[1], (BATCH, D_MODEL), 1.0),
        "w_ada": nrm(ks[2], (DEPTH, D_MODEL, 6 * D_MODEL), 0.5 * D_MODEL ** -0.5),
        "b_ada": nrm(ks[3], (DEPTH, 6 * D_MODEL), 0.01),
        "norm_g": 1.0 + nrm(ks[4], (DEPTH, 4, D_MODEL), 0.05),
        "w_in": nrm(ks[5], (DEPTH, D_MODEL, N_IN), D_MODEL ** -0.5),
        "w_a2": nrm(ks[6], (DEPTH, GLA_LOWRANK, GLA_QK), GLA_LOWRANK ** -0.5),
        "b_a2": nrm(ks[7], (DEPTH, GLA_QK), 0.1),
        "gla_norm_g": 1.0 + nrm(ks[8], (DEPTH, GLA_DV), 0.05),
        "w_out_gla": nrm(ks[9], (DEPTH, GLA_V, D_MODEL), GLA_V ** -0.5),
        "conv_mix_w": nrm(ks[10], (DEPTH, CONV_K, CONV_WIDTH), 0.5),
        "w_out_conv": nrm(ks[11], (DEPTH, CONV_WIDTH, D_MODEL), CONV_WIDTH ** -0.5),
        "w_o": nrm(ks[12], (DEPTH, D_MODEL, D_MODEL), D_MODEL ** -0.5),
        "w_up": nrm(ks[13], (DEPTH, D_MODEL, 2 * D_FF), D_MODEL ** -0.5),
        "ffn_conv_w": nrm(ks[14], (DEPTH, CONV_K, D_FF), 0.5),
        "w_down": nrm(ks[15], (DEPTH, D_FF, D_MODEL), D_FF ** -0.5),
    }


def reference(x, c, w_ada, b_ada, norm_g, w_in, w_a2, b_a2, gla_norm_g, w_out_gla,
              conv_mix_w, w_out_conv, w_o, w_up, ffn_conv_w, w_down):
    for l in range(DEPTH):
        mod = jax.nn.silu(c) @ w_ada[l] + b_ada[l]
        sh1, sc1, g1, sh2, sc2, g2 = [t[:, None, :] for t in jnp.split(mod, 6, axis=-1)]
        h = rmsnorm(x, norm_g[l, 0]) * (1.0 + sc1) + sh1
        y = token_mixer(h, w_in[l], w_a2[l], b_a2[l], gla_norm_g[l], w_out_gla[l],
                        conv_mix_w[l], w_out_conv[l], w_o[l])
        x = x + g1 * rmsnorm(y, norm_g[l, 1])
        h = rmsnorm(x, norm_g[l, 2]) * (1.0 + sc2) + sh2
        y = channel_mixer(h, w_up[l], ffn_conv_w[l], w_down[l])
        x = x + g2 * rmsnorm(y, norm_g[l, 3])
    return x
```

```python
import functools

import jax
import jax.numpy as jnp
from jax import lax
from jax.experimental import pallas as pl
from jax.experimental.pallas import tpu as pltpu

CHUNK = 64
GLA_HEADS = 4
GLA_LOWRANK = 16
GLA_TAU = 16.0
CONV_K = 3
EPS = 1e-6

LANES = 128
SUBLANES = 8
VMEM_LIMIT_BYTES = 56 * 1024 * 1024

_F32 = jnp.float32
_BF16 = jnp.bfloat16


def _rmsnorm(x, g):
    return x * lax.rsqrt(jnp.mean(x * x, axis=-1, keepdims=True) + EPS) * g


def _dot(a, b):
    return jnp.dot(a, b, preferred_element_type=_F32)


def _split3(a):
    a1 = a.astype(_BF16)
    r1 = a - a1.astype(_F32)
    a2 = r1.astype(_BF16)
    a3 = (r1 - a2.astype(_F32)).astype(_BF16)
    return a1, a2, a3


def _mod_kernel(c_ref, w_ref, b_ref, o_ref):
    c = c_ref[...]
    s = (c * jax.nn.sigmoid(c)).astype(_BF16)
    o_ref[...] = _dot(s, w_ref[...].astype(_BF16)) + b_ref[...]


def _modulation(c, w_ada, b_ada, tn=1024):
    depth, d, n = w_ada.shape
    b = c.shape[0]
    rows = -(-b // SUBLANES) * SUBLANES
    c_pad = jnp.zeros((rows, d), _F32).at[:b].set(c)
    out = pl.pallas_call(
        _mod_kernel,
        grid=(depth, n // tn),
        in_specs=[
            pl.BlockSpec((rows, d), lambda l, j: (0, 0)),
            pl.BlockSpec((None, d, tn), lambda l, j: (l, 0, j)),
            pl.BlockSpec((None, 1, tn), lambda l, j: (l, 0, j)),
        ],
        out_specs=pl.BlockSpec((None, rows, tn), lambda l, j: (l, 0, j)),
        out_shape=jax.ShapeDtypeStruct((depth, rows, n), _F32),
        compiler_params=pltpu.CompilerParams(
            dimension_semantics=("arbitrary", "arbitrary"),
            vmem_limit_bytes=VMEM_LIMIT_BYTES),
        name="adaln_mod",
    )(c_pad, w_ada, b_ada.reshape(depth, 1, n))
    return out[:, :b]


def _proj_kernel(x_ref, sc_ref, sh_ref, g_ref, w_ref, wlr_ref, o_ref, lr_ref, h_ref):
    @pl.when(pl.program_id(1) == 0)
    def _():
        h = _rmsnorm(x_ref[...], g_ref[...]) * (1.0 + sc_ref[...]) + sh_ref[...]
        h_ref[...] = h.astype(_BF16)
        lr_ref[...] = _dot(h_ref[...], wlr_ref[...])

    o_ref[...] = _dot(h_ref[...], w_ref[...])


def _proj(x, mod, g, w, wlr, seq, tm=512, tn=1024):
    m, d = x.shape
    n = w.shape[1]
    tps = seq // tm
    mod_spec = lambda idx: pl.BlockSpec((None, None, 1, d), lambda i, j: (i // tps, idx, 0, 0))
    return pl.pallas_call(
        _proj_kernel,
        grid=(m // tm, n // tn),
        in_specs=[
            pl.BlockSpec((tm, d), lambda i, j: (i, 0)),
            mod_spec(1), mod_spec(0),
            pl.BlockSpec((1, d), lambda i, j: (0, 0)),
            pl.BlockSpec((d, tn), lambda i, j: (0, j)),
            pl.BlockSpec((d, LANES), lambda i, j: (0, 0)),
        ],
        out_specs=[
            pl.BlockSpec((tm, tn), lambda i, j: (i, j)),
            pl.BlockSpec((tm, LANES), lambda i, j: (i, 0)),
        ],
        out_shape=[jax.ShapeDtypeStruct((m, n), _F32),
                   jax.ShapeDtypeStruct((m, LANES), _F32)],
        scratch_shapes=[pltpu.VMEM((tm, d), _BF16)],
        compiler_params=pltpu.CompilerParams(
            dimension_semantics=("arbitrary", "arbitrary"),
            vmem_limit_bytes=VMEM_LIMIT_BYTES),
        name="in_proj",
    )(x, mod, mod, g, w, wlr)


def _mixer_kernel(q_ref, k_ref, v_ref, r_ref, lr_ref, cb_ref, cc_ref, cx_ref, ga_ref, gb_ref,
                  wa2_ref, ba2_ref, gng_ref, cw_ref, wog_ref, woc_ref,
                  m_ref, state_ref, ucarry_ref, la_ref, o_ref, ubuf_ref, *, tps):
    tm = q_ref.shape[0]
    dk = q_ref.shape[1] // GLA_HEADS
    dv = v_ref.shape[1] // GLA_HEADS

    @pl.when(pl.program_id(0) % tps == 0)
    def _():
        state_ref[...] = jnp.zeros_like(state_ref)
        ucarry_ref[...] = jnp.zeros_like(ucarry_ref)

    lr1, lr2, _ = _split3(lr_ref[...])
    w1, w2, _ = _split3(wa2_ref[...])
    z = _dot(lr1, w1) + (_dot(lr2, w1) + _dot(lr1, w2)) + ba2_ref[...]
    la_ref[...] = (jnp.minimum(z, 0.0) - jnp.log1p(jnp.exp(-jnp.abs(z)))) / GLA_TAU

    row = lax.broadcasted_iota(jnp.int32, (CHUNK, CHUNK), 0)
    col = lax.broadcasted_iota(jnp.int32, (CHUNK, CHUNK), 1)
    upper = (col > row).astype(_BF16)
    scale = dk ** -0.5
    gng = gng_ref[...]

    def chunk_step(c, carry):
        rows = pl.ds(pl.multiple_of(c * CHUNK, CHUNK), CHUNK)
        a = la_ref[rows, :]
        a1, a2, a3 = _split3(a)
        rev = _dot(upper, a1) + (_dot(upper, a2) + _dot(upper, a3))
        decay_all = jnp.exp(rev[0:1, :] + a[0:1, :])
        k_dec = (k_ref[rows, :] * jnp.exp(rev)).astype(_BF16)
        q = (q_ref[rows, :] * scale).astype(_BF16)
        v = v_ref[rows, :].astype(_BF16)
        for h in range(GLA_HEADS):
            ks = slice(h * dk, (h + 1) * dk)
            vs = slice(h * dv, (h + 1) * dv)
            upd = lax.dot_general(v[:, vs], k_dec[:, ks], (((0,), (0,)), ((), ())),
                                  preferred_element_type=_F32)
            st = state_ref[h] * decay_all[:, ks] + upd
            state_ref[h] = st
            o = lax.dot_general(q[:, ks], st.astype(_BF16), (((1,), (1,)), ((), ())),
                                preferred_element_type=_F32)
            o_ref[rows, vs] = _rmsnorm(o, gng)
        return carry

    lax.fori_loop(0, tm // CHUNK, chunk_step, 0)

    r = r_ref[...]
    og = (o_ref[...] * (r * jax.nn.sigmoid(r))).astype(_BF16)
    y_a = _dot(og, wog_ref[...])

    u = cc_ref[...] * cx_ref[...]
    ubuf_ref[0:SUBLANES, :] = ucarry_ref[...]
    ubuf_ref[SUBLANES:, :] = u
    ucarry_ref[...] = u[tm - SUBLANES:, :]
    cw = cw_ref[...]
    conv = (cw[0:1] * ubuf_ref[pl.ds(SUBLANES - 2, tm), :]
            + cw[1:2] * ubuf_ref[pl.ds(SUBLANES - 1, tm), :]
            + cw[2:3] * u)
    y_b = _dot((cb_ref[...] * conv).astype(_BF16), woc_ref[...])

    m = jax.nn.sigmoid(ga_ref[...]) * y_a + jax.nn.sigmoid(gb_ref[...]) * y_b
    m_ref[...] = m.astype(_BF16)


def _mixer(p, lr, wa2, ba2, gng, cw, wog, woc, seq, d, tm=256):
    m = p.shape[0]
    qk = wa2.shape[1]
    gv = wog.shape[0]
    cwid = woc.shape[0]
    dk, dv = qk // GLA_HEADS, gv // GLA_HEADS
    tps = seq // tm
    col = lambda width, idx: pl.BlockSpec((tm, width), lambda i: (i, idx))
    const = lambda shape: pl.BlockSpec(shape, lambda i: (0,) * len(shape),
                                       pipeline_mode=pl.Buffered(1))
    assert gv == 2 * qk and cwid == gv and d == 2 * gv
    return pl.pallas_call(
        functools.partial(_mixer_kernel, tps=tps),
        grid=(m // tm,),
        in_specs=[
            col(qk, 0), col(qk, 1), col(gv, 1), col(gv, 2),
            pl.BlockSpec((tm, LANES), lambda i: (i, 0)),
            col(cwid, 3), col(cwid, 4), col(cwid, 5), col(d, 3), col(d, 4),
            const((LANES, qk)), const((1, qk)), const((1, dv)), const((CONV_K, cwid)),
            const((gv, d)), const((cwid, d)),
        ],
        out_specs=pl.BlockSpec((tm, d), lambda i: (i, 0)),
        out_shape=jax.ShapeDtypeStruct((m, d), _BF16),
        scratch_shapes=[
            pltpu.VMEM((GLA_HEADS, dv, dk), _F32),
            pltpu.VMEM((SUBLANES, cwid), _F32),
            pltpu.VMEM((tm, qk), _F32),
            pltpu.VMEM((tm, gv), _F32),
            pltpu.VMEM((tm + SUBLANES, cwid), _F32),
        ],
        compiler_params=pltpu.CompilerParams(
            dimension_semantics=("arbitrary",),
            vmem_limit_bytes=VMEM_LIMIT_BYTES),
        name="token_mixer",
    )(p, p, p, p, lr, p, p, p, p, p, wa2, ba2, gng, cw, wog, woc)


def _oproj_kernel(m_ref, x_ref, w_ref, g_ref, gate_ref, o_ref):
    y = _dot(m_ref[...], w_ref[...])
    o_ref[...] = x_ref[...] + gate_ref[...] * _rmsnorm(y, g_ref[...])


def _oproj(mm, x, w, g, mod, gate_idx, seq, tm=512):
    m, d = x.shape
    tps = seq // tm
    return pl.pallas_call(
        _oproj_kernel,
        grid=(m // tm,),
        in_specs=[
            pl.BlockSpec((tm, d), lambda i: (i, 0)),
            pl.BlockSpec((tm, d), lambda i: (i, 0)),
            pl.BlockSpec((d, d), lambda i: (0, 0), pipeline_mode=pl.Buffered(1)),
            pl.BlockSpec((1, d), lambda i: (0, 0)),
            pl.BlockSpec((None, None, 1, d), lambda i: (i // tps, gate_idx, 0, 0)),
        ],
        out_specs=pl.BlockSpec((tm, d), lambda i: (i, 0)),
        out_shape=jax.ShapeDtypeStruct((m, d), _F32),
        compiler_params=pltpu.CompilerParams(
            dimension_semantics=("arbitrary",),
            vmem_limit_bytes=VMEM_LIMIT_BYTES),
        name="out_proj",
    )(mm, x, w, g, mod)


def _gelu_tanh(x):
    return 0.5 * x * (1.0 + jnp.tanh(0.7978845608028654 * (x + 0.044715 * (x * x * x))))


def _ffn_kernel(x_ref, sc_ref, sh_ref, gate_ref, gin_ref, gout_ref, wg_ref, wu_ref, cw_ref, wd_ref,
                o_ref, h_ref, acc_ref, gbuf_ref, carry_ref, *, tps):
    i, j = pl.program_id(0), pl.program_id(1)
    tm = x_ref.shape[0]

    @pl.when(j == 0)
    def _():
        h = _rmsnorm(x_ref[...], gin_ref[...]) * (1.0 + sc_ref[...]) + sh_ref[...]
        h_ref[...] = h.astype(_BF16)

    h = h_ref[...]
    g = _dot(h, wg_ref[...])
    up = _dot(h, wu_ref[...])

    @pl.when(i % tps == 0)
    def _():
        gbuf_ref[0:SUBLANES, :] = jnp.zeros((SUBLANES, g.shape[1]), _F32)

    @pl.when(i % tps != 0)
    def _():
        gbuf_ref[0:SUBLANES, :] = carry_ref[j]

    gbuf_ref[SUBLANES:, :] = g
    carry_ref[j] = g[tm - SUBLANES:, :]
    cw = cw_ref[...]
    conv = (cw[0:1] * gbuf_ref[pl.ds(SUBLANES - 2, tm), :]
            + cw[1:2] * gbuf_ref[pl.ds(SUBLANES - 1, tm), :]
            + cw[2:3] * g)
    z = (_gelu_tanh(conv) * up).astype(_BF16)
    part = _dot(z, wd_ref[...])

    @pl.when(j == 0)
    def _():
        acc_ref[...] = part

    @pl.when(j > 0)
    def _():
        acc_ref[...] += part

    @pl.when(j == pl.num_programs(1) - 1)
    def _():
        o_ref[...] = x_ref[...] + gate_ref[...] * _rmsnorm(acc_ref[...], gout_ref[...])


def _ffn(x, mod, gin, gout, w_up, cw, w_down, seq, tm=512, tn=512):
    m, d = x.shape
    dff = w_down.shape[0]
    nj = dff // tn
    tps = seq // tm
    mod_spec = lambda idx: pl.BlockSpec((None, None, 1, d), lambda i, j: (i // tps, idx, 0, 0))
    vec = pl.BlockSpec((1, d), lambda i, j: (0, 0))
    return pl.pallas_call(
        functools.partial(_ffn_kernel, tps=tps),
        grid=(m // tm, nj),
        in_specs=[
            pl.BlockSpec((tm, d), lambda i, j: (i, 0)),
            mod_spec(4), mod_spec(3), mod_spec(5), vec, vec,
            pl.BlockSpec((d, tn), lambda i, j: (0, j)),
            pl.BlockSpec((d, tn), lambda i, j: (0, j + nj)),
            pl.BlockSpec((CONV_K, tn), lambda i, j: (0, j)),
            pl.BlockSpec((tn, d), lambda i, j: (j, 0)),
        ],
        out_specs=pl.BlockSpec((tm, d), lambda i, j: (i, 0)),
        out_shape=jax.ShapeDtypeStruct((m, d), _F32),
        scratch_shapes=[
            pltpu.VMEM((tm, d), _BF16),
            pltpu.VMEM((tm, d), _F32),
            pltpu.VMEM((tm + SUBLANES, tn), _F32),
            pltpu.VMEM((nj, SUBLANES, tn), _F32),
        ],
        compiler_params=pltpu.CompilerParams(
            dimension_semantics=("arbitrary", "arbitrary"),
            vmem_limit_bytes=VMEM_LIMIT_BYTES),
        name="conv_ffn",
    )(x, mod, mod, mod, gin, gout, w_up, w_up, cw, w_down)


def kernel(x, c, w_ada, b_ada, norm_g, w_in, w_a2, b_a2, gla_norm_g, w_out_gla, conv_mix_w,
           w_out_conv, w_o, w_up, ffn_conv_w, w_down):
    bsz, seq, d = x.shape
    depth = w_ada.shape[0]
    qk = w_a2.shape[2]
    gv = w_out_gla.shape[1]
    lr0 = 2 * qk + 2 * gv

    mod = _modulation(c, w_ada, b_ada)
    xf = x.reshape(bsz * seq, d)
    for l in range(depth):
        modl = mod[l].reshape(bsz, 6, 1, d)
        w_main = jnp.concatenate([w_in[l, :, :lr0], w_in[l, :, lr0 + GLA_LOWRANK:]],
                                 axis=1).astype(_BF16)
        w_lr = jnp.zeros((d, LANES), _BF16).at[:, :GLA_LOWRANK].set(
            w_in[l, :, lr0:lr0 + GLA_LOWRANK].astype(_BF16))
        wa2 = jnp.zeros((LANES, qk), _F32).at[:GLA_LOWRANK].set(w_a2[l])

        p, lr = _proj(xf, modl, norm_g[l, 0:1], w_main, w_lr, seq)
        mm = _mixer(p, lr, wa2, b_a2[l].reshape(1, qk), gla_norm_g[l].reshape(1, -1),
                    conv_mix_w[l], w_out_gla[l].astype(_BF16), w_out_conv[l].astype(_BF16), seq, d)
        xf = _oproj(mm, xf, w_o[l].astype(_BF16), norm_g[l, 1:2], modl, 2, seq)
        xf = _ffn(xf, modl, norm_g[l, 2:3], norm_g[l, 3:4], w_up[l].astype(_BF16),
                  ffn_conv_w[l], w_down[l].astype(_BF16), seq)
    return xf.reshape(bsz, seq, d)
```

```python
import functools

import jax
import jax.numpy as jnp
from jax import lax
from jax.experimental import pallas as pl
from jax.experimental.pallas import tpu as pltpu

CHUNK = 64
GLA_HEADS = 4
GLA_LOWRANK = 16
GLA_TAU = 16.0
CONV_K = 3
EPS = 1e-6

LANES = 128
SUBLANES = 8
MXU_COLS = 256
GATE_ROWS = 64
VMEM_LIMIT_BYTES = 56 * 1024 * 1024

_F32 = jnp.float32
_BF16 = jnp.bfloat16


def _rmsnorm(x, g):
    return x * lax.rsqrt(jnp.mean(x * x, axis=-1, keepdims=True) + EPS) * g


def _dot(a, b):
    return jnp.dot(a, b, preferred_element_type=_F32)


def _split3(a):
    a1 = a.astype(_BF16)
    r1 = a - a1.astype(_F32)
    a2 = r1.astype(_BF16)
    a3 = (r1 - a2.astype(_F32)).astype(_BF16)
    return a1, a2, a3


def _params(*semantics):
    return pltpu.CompilerParams(dimension_semantics=semantics, vmem_limit_bytes=VMEM_LIMIT_BYTES)


def _mod_kernel(c_ref, w_ref, b_ref, o_ref):
    c = c_ref[...]
    s = (c * jax.nn.sigmoid(c)).astype(_BF16)
    o_ref[...] = _dot(s, w_ref[...].astype(_BF16)) + b_ref[...]


def _modulation(c, w_ada, b_ada, tn=1024):
    depth, d, n = w_ada.shape
    b = c.shape[0]
    rows = -(-b // SUBLANES) * SUBLANES
    c_pad = jnp.pad(c, ((0, rows - b), (0, 0)))
    out = pl.pallas_call(
        _mod_kernel,
        grid=(depth, n // tn),
        in_specs=[
            pl.BlockSpec((rows, d), lambda l, j: (0, 0)),
            pl.BlockSpec((None, d, tn), lambda l, j: (l, 0, j)),
            pl.BlockSpec((None, 1, tn), lambda l, j: (l, 0, j)),
        ],
        out_specs=pl.BlockSpec((None, rows, tn), lambda l, j: (l, 0, j)),
        out_shape=jax.ShapeDtypeStruct((depth, rows, n), _F32),
        compiler_params=_params("arbitrary", "arbitrary"),
        name="adaln_mod",
    )(c_pad, w_ada, b_ada.reshape(depth, 1, n))
    return out[:, :b].reshape(depth, b, 6, 1, d)


def _proj_kernel(x_ref, sc_ref, sh_ref, g_ref, wa_ref, wb_ref, wlr_ref, o_ref, lr_ref, h_ref, *, na):
    j = pl.program_id(1)

    @pl.when(j == 0)
    def _():
        h = _rmsnorm(x_ref[...], g_ref[...]) * (1.0 + sc_ref[...]) + sh_ref[...]
        h_ref[...] = h.astype(_BF16)
        lr_ref[...] = _dot(h_ref[...], wlr_ref[...])

    @pl.when(j < na)
    def _():
        o_ref[...] = _dot(h_ref[...], wa_ref[...])

    @pl.when(j >= na)
    def _():
        o_ref[...] = _dot(h_ref[...], wb_ref[...])


def _proj(x, mod, norm_g, wa, wb, wlr, layer, seq, tm=1024, tn=1024):
    m, d = x.shape
    na, nb = wa.shape[2] // tn, wb.shape[2] // tn
    tps = seq // tm
    mod_spec = lambda idx: pl.BlockSpec((None, None, None, 1, d),
                                        lambda i, j: (layer, i // tps, idx, 0, 0))
    return pl.pallas_call(
        functools.partial(_proj_kernel, na=na),
        grid=(m // tm, na + nb),
        in_specs=[
            pl.BlockSpec((tm, d), lambda i, j: (i, 0)),
            mod_spec(1), mod_spec(0),
            pl.BlockSpec((None, None, 1, d), lambda i, j: (layer, 0, 0, 0)),
            pl.BlockSpec((None, d, tn), lambda i, j: (layer, 0, jnp.minimum(j, na - 1))),
            pl.BlockSpec((None, d, tn), lambda i, j: (layer, 0, jnp.maximum(j - na, 0))),
            pl.BlockSpec((None, d, LANES), lambda i, j: (layer, 0, 0)),
        ],
        out_specs=[
            pl.BlockSpec((tm, tn), lambda i, j: (i, j)),
            pl.BlockSpec((tm, LANES), lambda i, j: (i, 0)),
        ],
        out_shape=[jax.ShapeDtypeStruct((m, (na + nb) * tn), _F32),
                   jax.ShapeDtypeStruct((m, LANES), _F32)],
        scratch_shapes=[pltpu.VMEM((tm, d), _BF16)],
        compiler_params=_params("arbitrary", "arbitrary"),
        name="in_proj",
    )(x, mod, mod, norm_g, wa, wb, wlr)


def _mixer_kernel(q_ref, k_ref, v_ref, r_ref, lr_ref, cb_ref, cc_ref, cx_ref, ga_ref, gb_ref,
                  wa2_ref, ba2_ref, gng_ref, cw_ref, wog_ref, woc_ref,
                  m_ref, state_ref, ucarry_ref, la_ref, o_ref, ubuf_ref, *, tps):
    tm = q_ref.shape[0]
    dk = q_ref.shape[1] // GLA_HEADS
    dv = v_ref.shape[1] // GLA_HEADS

    @pl.when(pl.program_id(0) % tps == 0)
    def _():
        state_ref[...] = jnp.zeros_like(state_ref)
        ucarry_ref[...] = jnp.zeros_like(ucarry_ref)

    lr1, lr2, _ = _split3(lr_ref[...])
    w1, w2, _ = _split3(wa2_ref[...])
    z = _dot(lr1, w1) + (_dot(lr2, w1) + _dot(lr1, w2)) + ba2_ref[...]
    la_ref[...] = (jnp.minimum(z, 0.0) - jnp.log1p(jnp.exp(-jnp.abs(z)))) / GLA_TAU

    row = lax.broadcasted_iota(jnp.int32, (CHUNK, CHUNK), 0)
    col = lax.broadcasted_iota(jnp.int32, (CHUNK, CHUNK), 1)
    upper = (col > row).astype(_BF16)
    scale = dk ** -0.5
    gng = gng_ref[...]

    def chunk_step(c, carry):
        rows = pl.ds(pl.multiple_of(c * CHUNK, CHUNK), CHUNK)
        a = la_ref[rows, :]
        a1, a2, a3 = _split3(a)
        rev = _dot(upper, a1) + (_dot(upper, a2) + _dot(upper, a3))
        decay_all = jnp.exp(rev[0:1, :] + a[0:1, :])
        k_dec = (k_ref[rows, :] * jnp.exp(rev)).astype(_BF16)
        q = (q_ref[rows, :] * scale).astype(_BF16)
        v = v_ref[rows, :].astype(_BF16)
        for h in range(GLA_HEADS):
            ks = slice(h * dk, (h + 1) * dk)
            vs = slice(h * dv, (h + 1) * dv)
            upd = lax.dot_general(v[:, vs], k_dec[:, ks], (((0,), (0,)), ((), ())),
                                  preferred_element_type=_F32)
            st = state_ref[h] * decay_all[:, ks] + upd
            state_ref[h] = st
            o = lax.dot_general(q[:, ks], st.astype(_BF16), (((1,), (1,)), ((), ())),
                                preferred_element_type=_F32)
            o_ref[rows, vs] = _rmsnorm(o, gng)
        return carry

    lax.fori_loop(0, tm // CHUNK, chunk_step, 0)

    r = r_ref[...]
    og = (o_ref[...] * (r * jax.nn.sigmoid(r))).astype(_BF16)
    y_a = _dot(og, wog_ref[...])

    u = cc_ref[...] * cx_ref[...]
    ubuf_ref[0:SUBLANES, :] = ucarry_ref[...]
    ubuf_ref[SUBLANES:, :] = u
    ucarry_ref[...] = u[tm - SUBLANES:, :]
    cw = cw_ref[...]
    conv = (cw[0:1] * ubuf_ref[pl.ds(SUBLANES - 2, tm), :]
            + cw[1:2] * ubuf_ref[pl.ds(SUBLANES - 1, tm), :]
            + cw[2:3] * u)
    y_b = _dot((cb_ref[...] * conv).astype(_BF16), woc_ref[...])

    m = jax.nn.sigmoid(ga_ref[...]) * y_a + jax.nn.sigmoid(gb_ref[...]) * y_b
    m_ref[...] = m.astype(_BF16)


def _mixer(p, lr, wa2, ba2, gng, cw, wog, woc, layer, seq, tm=256):
    m = p.shape[0]
    qk = wa2.shape[2]
    gv, d = wog.shape[1:]
    cwid = woc.shape[1]
    dk, dv = qk // GLA_HEADS, gv // GLA_HEADS
    tps = seq // tm
    col = lambda width, idx: pl.BlockSpec((tm, width), lambda i: (i, idx))
    const = lambda *shape: pl.BlockSpec((None,) + shape, lambda i: (layer,) + (0,) * len(shape),
                                        pipeline_mode=pl.Buffered(1))
    assert gv == 2 * qk and cwid == gv and d == 2 * gv
    return pl.pallas_call(
        functools.partial(_mixer_kernel, tps=tps),
        grid=(m // tm,),
        in_specs=[
            col(qk, 0), col(qk, 1), col(gv, 1), col(gv, 2),
            pl.BlockSpec((tm, LANES), lambda i: (i, 0)),
            col(cwid, 3), col(cwid, 4), col(cwid, 5), col(d, 3), col(d, 4),
            const(LANES, qk), const(1, qk), const(1, dv), const(CONV_K, cwid),
            const(gv, d), const(cwid, d),
        ],
        out_specs=pl.BlockSpec((tm, d), lambda i: (i, 0)),
        out_shape=jax.ShapeDtypeStruct((m, d), _BF16),
        scratch_shapes=[
            pltpu.VMEM((GLA_HEADS, dv, dk), _F32),
            pltpu.VMEM((SUBLANES, cwid), _F32),
            pltpu.VMEM((tm, qk), _F32),
            pltpu.VMEM((tm, gv), _F32),
            pltpu.VMEM((tm + SUBLANES, cwid), _F32),
        ],
        compiler_params=_params("arbitrary"),
        name="token_mixer",
    )(p, p, p, p, lr, p, p, p, p, p, wa2, ba2, gng, cw, wog, woc)


def _oproj_kernel(m_ref, x_ref, w_ref, g_ref, gate_ref, o_ref):
    y = _dot(m_ref[...], w_ref[...])
    o_ref[...] = x_ref[...] + gate_ref[...] * _rmsnorm(y, g_ref[...])


def _oproj(mm, x, w, norm_g, mod, layer, seq, tm=512):
    m, d = x.shape
    tps = seq // tm
    return pl.pallas_call(
        _oproj_kernel,
        grid=(m // tm,),
        in_specs=[
            pl.BlockSpec((tm, d), lambda i: (i, 0)),
            pl.BlockSpec((tm, d), lambda i: (i, 0)),
            pl.BlockSpec((None, d, d), lambda i: (layer, 0, 0), pipeline_mode=pl.Buffered(1)),
            pl.BlockSpec((None, None, 1, d), lambda i: (layer, 1, 0, 0)),
            pl.BlockSpec((None, None, None, 1, d), lambda i: (layer, i // tps, 2, 0, 0)),
        ],
        out_specs=pl.BlockSpec((tm, d), lambda i: (i, 0)),
        out_shape=jax.ShapeDtypeStruct((m, d), _F32),
        compiler_params=_params("arbitrary"),
        name="out_proj",
    )(mm, x, w, norm_g, mod)


def _gelu_tanh(x):
    return 0.5 * x * (1.0 + jnp.tanh(0.7978845608028654 * (x + 0.044715 * (x * x * x))))


def _ffn_kernel(x_ref, sc_ref, sh_ref, gate_ref, gin_ref, gout_ref, wg_ref, wu_ref, cw_ref, wd_ref,
                o_ref, h_ref, z_ref, g0_ref, g1_ref, u0_ref, u1_ref, carry_ref, *, tps, nj):
    i, s = pl.program_id(0), pl.program_id(1)
    tm = x_ref.shape[0]
    tn = wg_ref.shape[1]
    tnd = wd_ref.shape[1]
    gbufs, ubufs = (g0_ref, g1_ref), (u0_ref, u1_ref)

    def up_project(slot):
        h = h_ref[...]
        gbufs[slot][SUBLANES:, :] = _dot(h, wg_ref[...])
        ubufs[slot][...] = _dot(h, wu_ref[...])

    def gate(slot, j):
        gbuf = gbufs[slot]

        gbuf[0:SUBLANES, :] = carry_ref[j]
        carry_ref[j] = gbuf[tm:, :]
        cw = cw_ref[...]
        cols = (pl.ds(j * tn, tn) if isinstance(j, int)
                else pl.ds(pl.multiple_of(j * tn, tn), tn))
        for r in range(0, tm, GATE_ROWS):
            conv = (cw[0:1] * gbuf[pl.ds(r + SUBLANES - 2, GATE_ROWS), :]
                    + cw[1:2] * gbuf[pl.ds(r + SUBLANES - 1, GATE_ROWS), :]
                    + cw[2:3] * gbuf[pl.ds(r + SUBLANES, GATE_ROWS), :])
            z = _gelu_tanh(conv) * ubufs[slot][pl.ds(r, GATE_ROWS), :]
            z_ref[pl.ds(r, GATE_ROWS), cols] = z.astype(_BF16)

    @pl.when((s == 0) & (i % tps == 0))
    def _():
        carry_ref[...] = jnp.zeros_like(carry_ref)

    @pl.when(s == 0)
    def _():
        h = _rmsnorm(x_ref[...], gin_ref[...]) * (1.0 + sc_ref[...]) + sh_ref[...]
        h_ref[...] = h.astype(_BF16)
        up_project(0)

    for parity in (0, 1):
        @pl.when((s >= 1) & (s < nj) & (s % 2 == parity))
        def _():
            gate(1 - parity, s - 1)
            up_project(parity)

    @pl.when(s == nj)
    def _():
        gate((nj - 1) % 2, nj - 1)

    @pl.when(s >= nj)
    def _():
        n = s - nj
        o_ref[:, pl.ds(pl.multiple_of(n * tnd, tnd), tnd)] = _dot(z_ref[...], wd_ref[...])

    @pl.when(s == pl.num_programs(1) - 1)
    def _():
        o_ref[...] = x_ref[...] + gate_ref[...] * _rmsnorm(o_ref[...], gout_ref[...])


def _ffn(x, mod, norm_g, w_up, cw, w_down, layer, seq, tm=512, tn=512, tnd=MXU_COLS):
    m, d = x.shape
    dff = w_down.shape[1]
    nj, nn = dff // tn, d // tnd
    tps = seq // tm
    mod_spec = lambda idx: pl.BlockSpec((None, None, None, 1, d),
                                        lambda i, s: (layer, i // tps, idx, 0, 0))
    vec = lambda idx: pl.BlockSpec((None, None, 1, d), lambda i, s: (layer, idx, 0, 0))
    return pl.pallas_call(
        functools.partial(_ffn_kernel, tps=tps, nj=nj),
        grid=(m // tm, nj + nn),
        in_specs=[
            pl.BlockSpec((tm, d), lambda i, s: (i, 0)),
            mod_spec(4), mod_spec(3), mod_spec(5), vec(2), vec(3),
            pl.BlockSpec((None, d, tn), lambda i, s: (layer, 0, jnp.minimum(s, nj - 1))),
            pl.BlockSpec((None, d, tn), lambda i, s: (layer, 0, jnp.minimum(s, nj - 1) + nj)),
            pl.BlockSpec((None, CONV_K, tn), lambda i, s: (layer, 0, jnp.clip(s - 1, 0, nj - 1))),
            pl.BlockSpec((None, dff, tnd), lambda i, s: (layer, 0, jnp.maximum(s - nj, 0))),
        ],
        out_specs=pl.BlockSpec((tm, d), lambda i, s: (i, 0)),
        out_shape=jax.ShapeDtypeStruct((m, d), _F32),
        scratch_shapes=[
            pltpu.VMEM((tm, d), _BF16),
            pltpu.VMEM((tm, dff), _BF16),
            pltpu.VMEM((tm + SUBLANES, tn), _F32),
            pltpu.VMEM((tm + SUBLANES, tn), _F32),
            pltpu.VMEM((tm, tn), _F32),
            pltpu.VMEM((tm, tn), _F32),
            pltpu.VMEM((nj, SUBLANES, tn), _F32),
        ],
        compiler_params=_params("arbitrary", "arbitrary"),
        name="conv_ffn",
    )(x, mod, mod, mod, norm_g, norm_g, w_up, w_up, cw, w_down)


def kernel(x, c, w_ada, b_ada, norm_g, w_in, w_a2, b_a2, gla_norm_g, w_out_gla, conv_mix_w,
           w_out_conv, w_o, w_up, ffn_conv_w, w_down):
    bsz, seq, d = x.shape
    depth = w_ada.shape[0]
    qk = w_a2.shape[2]
    gv = w_out_gla.shape[1]
    lr0 = 2 * qk + 2 * gv

    mod = _modulation(c, w_ada, b_ada)
    w_in_a = w_in[:, :, :lr0].astype(_BF16)
    w_in_b = w_in[:, :, lr0 + GLA_LOWRANK:].astype(_BF16)
    w_lr = jnp.pad(w_in[:, :, lr0:lr0 + GLA_LOWRANK].astype(_BF16),
                   ((0, 0), (0, 0), (0, LANES - GLA_LOWRANK)))
    wa2 = jnp.pad(w_a2, ((0, 0), (0, LANES - GLA_LOWRANK), (0, 0)))
    w_og, w_oc, w_ob = w_out_gla.astype(_BF16), w_out_conv.astype(_BF16), w_o.astype(_BF16)
    w_upb, w_downb = w_up.astype(_BF16), w_down.astype(_BF16)
    ng = norm_g.reshape(depth, 4, 1, d)
    ba2 = b_a2.reshape(depth, 1, qk)
    gng = gla_norm_g.reshape(depth, 1, -1)

    xf = x.reshape(bsz * seq, d)
    for l in range(depth):
        p, lr = _proj(xf, mod, ng, w_in_a, w_in_b, w_lr, l, seq)
        mm = _mixer(p, lr, wa2, ba2, gng, conv_mix_w, w_og, w_oc, l, seq)
        xf = _oproj(mm, xf, w_ob, ng, mod, l, seq)
        xf = _ffn(xf, mod, ng, w_upb, ffn_conv_w, w_downb, l, seq)
    return xf.reshape(bsz, seq, d)
```

```python
import functools

import jax
import jax.numpy as jnp
from jax import lax
from jax.experimental import pallas as pl
from jax.experimental.pallas import tpu as pltpu

CHUNK = 64
GLA_HEADS = 4
GLA_LOWRANK = 16
GLA_TAU = 16.0
CONV_K = 3
EPS = 1e-6

LANES = 128
SUBLANES = 8
MXU_COLS = 256
GATE_ROWS = 64
PROJ_TN = 1024
FFN_TN = 512
VMEM_LIMIT_BYTES = 56 * 1024 * 1024

_F32 = jnp.float32
_BF16 = jnp.bfloat16


def _rmsnorm(x, g):
    return x * lax.rsqrt(jnp.mean(x * x, axis=-1, keepdims=True) + EPS) * g


def _sigmoid(x):
    return 0.5 * jnp.tanh(0.5 * x) + 0.5


def _dot(a, b):
    return jnp.dot(a, b, preferred_element_type=_F32)


def _split3(a):
    a1 = a.astype(_BF16)
    r1 = a - a1.astype(_F32)
    a2 = r1.astype(_BF16)
    a3 = (r1 - a2.astype(_F32)).astype(_BF16)
    return a1, a2, a3


def _params(*semantics):
    return pltpu.CompilerParams(dimension_semantics=semantics, vmem_limit_bytes=VMEM_LIMIT_BYTES)


def _mod_kernel(c_ref, w_ref, b_ref, o_ref):
    c = c_ref[...]
    s = (c * jax.nn.sigmoid(c)).astype(_BF16)
    o_ref[...] = _dot(s, w_ref[...].astype(_BF16)) + b_ref[...]


def _modulation(c, w_ada, b_ada, tn=1024):
    depth, d, n = w_ada.shape
    b = c.shape[0]
    rows = -(-b // SUBLANES) * SUBLANES
    c_pad = jnp.pad(c, ((0, rows - b), (0, 0)))
    out = pl.pallas_call(
        _mod_kernel,
        grid=(depth, n // tn),
        in_specs=[
            pl.BlockSpec((rows, d), lambda l, j: (0, 0)),
            pl.BlockSpec((None, d, tn), lambda l, j: (l, 0, j)),
            pl.BlockSpec((None, 1, tn), lambda l, j: (l, 0, j)),
        ],
        out_specs=pl.BlockSpec((None, rows, tn), lambda l, j: (l, 0, j)),
        out_shape=jax.ShapeDtypeStruct((depth, rows, n), _F32),
        compiler_params=_params("arbitrary", "arbitrary"),
        name="adaln_mod",
    )(c_pad, w_ada, b_ada.reshape(depth, 1, n))
    return out[:, :b].reshape(depth, b, 6, 1, d)


def _proj_kernel(x_ref, sc_ref, sh_ref, g_ref, wa_ref, wb_ref, wlr_ref, o_ref, lr_ref, h_ref, *, na):
    j = pl.program_id(1)

    @pl.when(j == 0)
    def _():
        h = _rmsnorm(x_ref[...], g_ref[...]) * (1.0 + sc_ref[...]) + sh_ref[...]
        h_ref[...] = h.astype(_BF16)
        lr_ref[...] = _dot(h_ref[...], wlr_ref[...])

    @pl.when(j < na)
    def _():
        o_ref[...] = _dot(h_ref[...], wa_ref[...])

    @pl.when(j >= na)
    def _():
        o_ref[...] = _dot(h_ref[...], wb_ref[...])


def _proj(x, mod, norm_g, wa, wb, wlr, layer, seq, tm=1024):
    m, d = x.shape
    na, nb, tn = wa.shape[1], wb.shape[1], wa.shape[3]
    tps = seq // tm
    mod_spec = lambda idx: pl.BlockSpec((None, None, None, 1, d),
                                        lambda i, j: (layer, i // tps, idx, 0, 0))
    return pl.pallas_call(
        functools.partial(_proj_kernel, na=na),
        grid=(m // tm, na + nb),
        in_specs=[
            pl.BlockSpec((tm, d), lambda i, j: (i, 0)),
            mod_spec(1), mod_spec(0),
            pl.BlockSpec((None, None, 1, d), lambda i, j: (layer, 0, 0, 0)),
            pl.BlockSpec((None, None, d, tn), lambda i, j: (layer, jnp.minimum(j, na - 1), 0, 0)),
            pl.BlockSpec((None, None, d, tn), lambda i, j: (layer, jnp.maximum(j - na, 0), 0, 0)),
            pl.BlockSpec((None, d, LANES), lambda i, j: (layer, 0, 0)),
        ],
        out_specs=[
            pl.BlockSpec((tm, tn), lambda i, j: (i, j)),
            pl.BlockSpec((tm, LANES), lambda i, j: (i, 0)),
        ],
        out_shape=[jax.ShapeDtypeStruct((m, (na + nb) * tn), _F32),
                   jax.ShapeDtypeStruct((m, LANES), _F32)],
        scratch_shapes=[pltpu.VMEM((tm, d), _BF16)],
        compiler_params=_params("arbitrary", "arbitrary"),
        name="in_proj",
    )(x, mod, mod, norm_g, wa, wb, wlr)


def _mixer_kernel(q_ref, k_ref, v_ref, r_ref, lr_ref, cb_ref, cc_ref, cx_ref, ga_ref, gb_ref,
                  wa2_ref, ba2_ref, gng_ref, cw_ref, wog_ref, woc_ref,
                  m_ref, state_ref, ucarry_ref, o_ref, og_ref, ubuf_ref, *, tps):
    tm = q_ref.shape[0]
    dk = q_ref.shape[1] // GLA_HEADS
    dv = v_ref.shape[1] // GLA_HEADS

    @pl.when(pl.program_id(0) % tps == 0)
    def _():
        state_ref[...] = jnp.zeros_like(state_ref)
        ucarry_ref[...] = jnp.zeros_like(ucarry_ref)

    lr1, lr2, _ = _split3(lr_ref[...])
    w1, w2, _ = _split3(wa2_ref[...])
    z = _dot(lr1, w1) + (_dot(lr2, w1) + _dot(lr1, w2)) + ba2_ref[...]
    la = (jnp.minimum(z, 0.0) - jnp.log1p(jnp.exp(-jnp.abs(z)))) / GLA_TAU

    row = lax.broadcasted_iota(jnp.int32, (tm, tm), 0)
    col = lax.broadcasted_iota(jnp.int32, (tm, tm), 1)
    upper = ((col > row) & ((col ^ row) < CHUNK)).astype(_BF16)
    a1, a2, a3 = _split3(la)
    rev = _dot(upper, a1) + (_dot(upper, a2) + _dot(upper, a3))
    k_dec = (k_ref[...] * jnp.exp(rev)).astype(_BF16)
    q = (q_ref[...] * (dk ** -0.5)).astype(_BF16)
    v = v_ref[...].astype(_BF16)

    chunks = [slice(c * CHUNK, (c + 1) * CHUNK) for c in range(tm // CHUNK)]
    heads = [(slice(h * dk, (h + 1) * dk), slice(h * dv, (h + 1) * dv)) for h in range(GLA_HEADS)]
    upd = [[lax.dot_general(v[rows, vs], k_dec[rows, ks], (((0,), (0,)), ((), ())),
                            preferred_element_type=_F32) for ks, vs in heads] for rows in chunks]
    st = [state_ref[h] for h in range(GLA_HEADS)]
    for c, rows in enumerate(chunks):
        decay_all = jnp.exp(rev[c * CHUNK:c * CHUNK + 1, :] + la[c * CHUNK:c * CHUNK + 1, :])
        for h, (ks, vs) in enumerate(heads):
            st[h] = st[h] * decay_all[:, ks] + upd[c][h]
            o_ref[rows, vs] = lax.dot_general(q[rows, ks], st[h].astype(_BF16),
                                              (((1,), (1,)), ((), ())),
                                              preferred_element_type=_F32)
    for h in range(GLA_HEADS):
        state_ref[h] = st[h]

    gng = gng_ref[...]
    for h in range(GLA_HEADS):
        vs = slice(h * dv, (h + 1) * dv)
        r = r_ref[:, vs]
        og_ref[:, vs] = (_rmsnorm(o_ref[:, vs], gng) * (r * _sigmoid(r))).astype(_BF16)
    y_a = _dot(og_ref[...], wog_ref[...])

    u = cc_ref[...] * cx_ref[...]
    ubuf_ref[0:SUBLANES, :] = ucarry_ref[...]
    ubuf_ref[SUBLANES:, :] = u
    ucarry_ref[...] = u[tm - SUBLANES:, :]
    cw = cw_ref[...]
    conv = (cw[0:1] * ubuf_ref[pl.ds(SUBLANES - 2, tm), :]
            + cw[1:2] * ubuf_ref[pl.ds(SUBLANES - 1, tm), :]
            + cw[2:3] * u)
    y_b = _dot((cb_ref[...] * conv).astype(_BF16), woc_ref[...])

    m = _sigmoid(ga_ref[...]) * y_a + _sigmoid(gb_ref[...]) * y_b
    m_ref[...] = m.astype(_BF16)


def _mixer(p, lr, wa2, ba2, gng, cw, wog, woc, layer, seq, tm=256):
    m = p.shape[0]
    qk = wa2.shape[2]
    gv, d = wog.shape[1:]
    cwid = woc.shape[1]
    dk, dv = qk // GLA_HEADS, gv // GLA_HEADS
    tps = seq // tm
    col = lambda width, idx: pl.BlockSpec((tm, width), lambda i: (i, idx))
    const = lambda *shape: pl.BlockSpec((None,) + shape, lambda i: (layer,) + (0,) * len(shape),
                                        pipeline_mode=pl.Buffered(1))
    assert gv == 2 * qk and cwid == gv and d == 2 * gv
    return pl.pallas_call(
        functools.partial(_mixer_kernel, tps=tps),
        grid=(m // tm,),
        in_specs=[
            col(qk, 0), col(qk, 1), col(gv, 1), col(gv, 2),
            pl.BlockSpec((tm, LANES), lambda i: (i, 0)),
            col(cwid, 3), col(cwid, 4), col(cwid, 5), col(d, 3), col(d, 4),
            const(LANES, qk), const(1, qk), const(1, dv), const(CONV_K, cwid),
            const(gv, d), const(cwid, d),
        ],
        out_specs=pl.BlockSpec((tm, d), lambda i: (i, 0)),
        out_shape=jax.ShapeDtypeStruct((m, d), _BF16),
        scratch_shapes=[
            pltpu.VMEM((GLA_HEADS, dv, dk), _F32),
            pltpu.VMEM((SUBLANES, cwid), _F32),
            pltpu.VMEM((tm, gv), _F32),
            pltpu.VMEM((tm, gv), _BF16),
            pltpu.VMEM((tm + SUBLANES, cwid), _F32),
        ],
        compiler_params=_params("arbitrary"),
        name="token_mixer",
    )(p, p, p, p, lr, p, p, p, p, p, wa2, ba2, gng, cw, wog, woc)


def _oproj_kernel(m_ref, x_ref, w_ref, g_ref, gate_ref, o_ref):
    y = _dot(m_ref[...], w_ref[...])
    o_ref[...] = x_ref[...] + gate_ref[...] * _rmsnorm(y, g_ref[...])


def _oproj(mm, x, w, norm_g, mod, layer, seq, tm=512):
    m, d = x.shape
    tps = seq // tm
    return pl.pallas_call(
        _oproj_kernel,
        grid=(m // tm,),
        in_specs=[
            pl.BlockSpec((tm, d), lambda i: (i, 0)),
            pl.BlockSpec((tm, d), lambda i: (i, 0)),
            pl.BlockSpec((None, d, d), lambda i: (layer, 0, 0), pipeline_mode=pl.Buffered(1)),
            pl.BlockSpec((None, None, 1, d), lambda i: (layer, 1, 0, 0)),
            pl.BlockSpec((None, None, None, 1, d), lambda i: (layer, i // tps, 2, 0, 0)),
        ],
        out_specs=pl.BlockSpec((tm, d), lambda i: (i, 0)),
        out_shape=jax.ShapeDtypeStruct((m, d), _F32),
        compiler_params=_params("arbitrary"),
        name="out_proj",
    )(mm, x, w, norm_g, mod)


def _gelu_tanh(x):
    return 0.5 * x * (1.0 + jnp.tanh(0.7978845608028654 * (x + 0.044715 * (x * x * x))))


def _ffn_kernel(x_ref, sc_ref, sh_ref, gate_ref, gin_ref, gout_ref, wg_ref, wu_ref, cw_ref, wd_ref,
                o_ref, h_ref, z_ref, g0_ref, g1_ref, u0_ref, u1_ref, carry_ref, *, tps, nj):
    i, s = pl.program_id(0), pl.program_id(1)
    tm = x_ref.shape[0]
    tn = wg_ref.shape[1]
    tnd = wd_ref.shape[1]
    gbufs, ubufs = (g0_ref, g1_ref), (u0_ref, u1_ref)

    def up_project(slot):
        h = h_ref[...]
        gbufs[slot][SUBLANES:, :] = _dot(h, wg_ref[...])
        ubufs[slot][...] = _dot(h, wu_ref[...])

    def gate(slot, j):
        gbuf = gbufs[slot]

        gbuf[0:SUBLANES, :] = carry_ref[j]
        carry_ref[j] = gbuf[tm:, :]
        cw = cw_ref[...]
        cols = (pl.ds(j * tn, tn) if isinstance(j, int)
                else pl.ds(pl.multiple_of(j * tn, tn), tn))
        for r in range(0, tm, GATE_ROWS):
            conv = (cw[0:1] * gbuf[pl.ds(r + SUBLANES - 2, GATE_ROWS), :]
                    + cw[1:2] * gbuf[pl.ds(r + SUBLANES - 1, GATE_ROWS), :]
                    + cw[2:3] * gbuf[pl.ds(r + SUBLANES, GATE_ROWS), :])
            z = _gelu_tanh(conv) * ubufs[slot][pl.ds(r, GATE_ROWS), :]
            z_ref[pl.ds(r, GATE_ROWS), cols] = z.astype(_BF16)

    @pl.when((s == 0) & (i % tps == 0))
    def _():
        carry_ref[...] = jnp.zeros_like(carry_ref)

    @pl.when(s == 0)
    def _():
        h = _rmsnorm(x_ref[...], gin_ref[...]) * (1.0 + sc_ref[...]) + sh_ref[...]
        h_ref[...] = h.astype(_BF16)
        up_project(0)

    for parity in (0, 1):
        @pl.when((s >= 1) & (s < nj) & (s % 2 == parity))
        def _():
            gate(1 - parity, s - 1)
            up_project(parity)

    @pl.when(s == nj)
    def _():
        gate((nj - 1) % 2, nj - 1)

    @pl.when(s >= nj)
    def _():
        n = s - nj
        o_ref[:, pl.ds(pl.multiple_of(n * tnd, tnd), tnd)] = _dot(z_ref[...], wd_ref[...])

    @pl.when(s == pl.num_programs(1) - 1)
    def _():
        o_ref[...] = x_ref[...] + gate_ref[...] * _rmsnorm(o_ref[...], gout_ref[...])


def _ffn(x, mod, norm_g, w_up, cw, w_down, layer, seq, tm=512):
    m, d = x.shape
    nj, tn = w_up.shape[1] // 2, w_up.shape[3]
    nn, dff, tnd = w_down.shape[1:]
    tps = seq // tm
    mod_spec = lambda idx: pl.BlockSpec((None, None, None, 1, d),
                                        lambda i, s: (layer, i // tps, idx, 0, 0))
    vec = lambda idx: pl.BlockSpec((None, None, 1, d), lambda i, s: (layer, idx, 0, 0))
    return pl.pallas_call(
        functools.partial(_ffn_kernel, tps=tps, nj=nj),
        grid=(m // tm, nj + nn),
        in_specs=[
            pl.BlockSpec((tm, d), lambda i, s: (i, 0)),
            mod_spec(4), mod_spec(3), mod_spec(5), vec(2), vec(3),
            pl.BlockSpec((None, None, d, tn), lambda i, s: (layer, jnp.minimum(s, nj - 1), 0, 0)),
            pl.BlockSpec((None, None, d, tn), lambda i, s: (layer, jnp.minimum(s, nj - 1) + nj, 0, 0)),
            pl.BlockSpec((None, CONV_K, tn), lambda i, s: (layer, 0, jnp.clip(s - 1, 0, nj - 1))),
            pl.BlockSpec((None, None, dff, tnd), lambda i, s: (layer, jnp.maximum(s - nj, 0), 0, 0)),
        ],
        out_specs=pl.BlockSpec((tm, d), lambda i, s: (i, 0)),
        out_shape=jax.ShapeDtypeStruct((m, d), _F32),
        scratch_shapes=[
            pltpu.VMEM((tm, d), _BF16),
            pltpu.VMEM((tm, dff), _BF16),
            pltpu.VMEM((tm + SUBLANES, tn), _F32),
            pltpu.VMEM((tm + SUBLANES, tn), _F32),
            pltpu.VMEM((tm, tn), _F32),
            pltpu.VMEM((tm, tn), _F32),
            pltpu.VMEM((nj, SUBLANES, tn), _F32),
        ],
        compiler_params=_params("arbitrary", "arbitrary"),
        name="conv_ffn",
    )(x, mod, mod, mod, norm_g, norm_g, w_up, w_up, cw, w_down)


def _col_blocks(w, tn):
    depth, k, n = w.shape
    return w.reshape(depth, k, n // tn, tn).transpose(0, 2, 1, 3).astype(_BF16)


def kernel(x, c, w_ada, b_ada, norm_g, w_in, w_a2, b_a2, gla_norm_g, w_out_gla, conv_mix_w,
           w_out_conv, w_o, w_up, ffn_conv_w, w_down):
    bsz, seq, d = x.shape
    depth = w_ada.shape[0]
    qk = w_a2.shape[2]
    gv = w_out_gla.shape[1]
    lr0 = 2 * qk + 2 * gv

    mod = _modulation(c, w_ada, b_ada)
    w_in_a = _col_blocks(w_in[:, :, :lr0], PROJ_TN)
    w_in_b = _col_blocks(w_in[:, :, lr0 + GLA_LOWRANK:], PROJ_TN)
    w_lr = jnp.pad(w_in[:, :, lr0:lr0 + GLA_LOWRANK].astype(_BF16),
                   ((0, 0), (0, 0), (0, LANES - GLA_LOWRANK)))
    wa2 = jnp.pad(w_a2, ((0, 0), (0, LANES - GLA_LOWRANK), (0, 0)))
    w_og, w_oc, w_ob = w_out_gla.astype(_BF16), w_out_conv.astype(_BF16), w_o.astype(_BF16)
    w_upb, w_downb = _col_blocks(w_up, FFN_TN), _col_blocks(w_down, MXU_COLS)
    ng = norm_g.reshape(depth, 4, 1, d)
    ba2 = b_a2.reshape(depth, 1, qk)
    gng = gla_norm_g.reshape(depth, 1, -1)

    xf = x.reshape(bsz * seq, d)
    for l in range(depth):
        p, lr = _proj(xf, mod, ng, w_in_a, w_in_b, w_lr, l, seq)
        mm = _mixer(p, lr, wa2, ba2, gng, conv_mix_w, w_og, w_oc, l, seq)
        xf = _oproj(mm, xf, w_ob, ng, mod, l, seq)
        xf = _ffn(xf, mod, ng, w_upb, ffn_conv_w, w_downb, l, seq)
    return xf.reshape(bsz, seq, d)
```

```python
import functools

import jax
import jax.numpy as jnp
from jax import lax
from jax.experimental import pallas as pl
from jax.experimental.pallas import tpu as pltpu

CHUNK = 64
GLA_HEADS = 4
GLA_LOWRANK = 16
GLA_TAU = 16.0
CONV_K = 3
EPS = 1e-6

LANES = 128
SUBLANES = 8
GATE_ROWS = 32
MIX_ROWS = 32
PROJ_TN = 1024
FFN_TN = 512
FFN_TND = 512
VMEM_LIMIT_BYTES = 56 * 1024 * 1024

_F32 = jnp.float32
_BF16 = jnp.bfloat16


def _rmsnorm(x, g):
    return x * lax.rsqrt(jnp.mean(x * x, axis=-1, keepdims=True) + EPS) * g


def _sigmoid(x):
    return 0.5 * jnp.tanh(0.5 * x) + 0.5


def _dot(a, b):
    return jnp.dot(a, b, preferred_element_type=_F32)


def _split3(a):
    a1 = a.astype(_BF16)
    r1 = a - a1.astype(_F32)
    a2 = r1.astype(_BF16)
    a3 = (r1 - a2.astype(_F32)).astype(_BF16)
    return a1, a2, a3


def _params(*semantics):
    return pltpu.CompilerParams(dimension_semantics=semantics, vmem_limit_bytes=VMEM_LIMIT_BYTES)


def _mod_kernel(c_ref, w_ref, b_ref, o_ref):
    c = c_ref[...]
    s = (c * jax.nn.sigmoid(c)).astype(_BF16)
    o_ref[...] = _dot(s, w_ref[...].astype(_BF16)) + b_ref[...]


def _modulation(c, w_ada, b_ada, tn=1024):
    depth, d, n = w_ada.shape
    b = c.shape[0]
    rows = -(-b // SUBLANES) * SUBLANES
    c_pad = jnp.pad(c, ((0, rows - b), (0, 0)))
    out = pl.pallas_call(
        _mod_kernel,
        grid=(depth, n // tn),
        in_specs=[
            pl.BlockSpec((rows, d), lambda l, j: (0, 0)),
            pl.BlockSpec((None, d, tn), lambda l, j: (l, 0, j)),
            pl.BlockSpec((None, 1, tn), lambda l, j: (l, 0, j)),
        ],
        out_specs=pl.BlockSpec((None, rows, tn), lambda l, j: (l, 0, j)),
        out_shape=jax.ShapeDtypeStruct((depth, rows, n), _F32),
        compiler_params=_params("arbitrary", "arbitrary"),
        name="adaln_mod",
    )(c_pad, w_ada, b_ada.reshape(depth, 1, n))
    return out[:, :b].reshape(depth, b, 6, 1, d)


def _proj_kernel(x_ref, sc_ref, sh_ref, g_ref, wa_ref, wb_ref, wlr_ref, o_ref, lr_ref, h_ref, *, na):
    j = pl.program_id(1)

    @pl.when(j == 0)
    def _():
        h = _rmsnorm(x_ref[...], g_ref[...]) * (1.0 + sc_ref[...]) + sh_ref[...]
        h_ref[...] = h.astype(_BF16)
        lr_ref[...] = _dot(h_ref[...], wlr_ref[...])

    @pl.when(j < na)
    def _():
        o_ref[...] = _dot(h_ref[...], wa_ref[...])

    @pl.when(j >= na)
    def _():
        o_ref[...] = _dot(h_ref[...], wb_ref[...])


def _proj(x, mod, norm_g, wa, wb, wlr, layer, seq, tm=1024, tn=PROJ_TN):
    m, d = x.shape
    na, nb = wa.shape[2] // tn, wb.shape[2] // tn
    tps = seq // tm
    mod_spec = lambda idx: pl.BlockSpec((None, None, None, 1, d),
                                        lambda i, j: (layer, i // tps, idx, 0, 0))
    return pl.pallas_call(
        functools.partial(_proj_kernel, na=na),
        grid=(m // tm, na + nb),
        in_specs=[
            pl.BlockSpec((tm, d), lambda i, j: (i, 0)),
            mod_spec(1), mod_spec(0),
            pl.BlockSpec((None, None, 1, d), lambda i, j: (layer, 0, 0, 0)),
            pl.BlockSpec((None, d, tn), lambda i, j: (layer, 0, jnp.minimum(j, na - 1))),
            pl.BlockSpec((None, d, tn), lambda i, j: (layer, 0, jnp.maximum(j - na, 0))),
            pl.BlockSpec((None, d, LANES), lambda i, j: (layer, 0, 0)),
        ],
        out_specs=[
            pl.BlockSpec((tm, tn), lambda i, j: (i, j)),
            pl.BlockSpec((tm, LANES), lambda i, j: (i, 0)),
        ],
        out_shape=[jax.ShapeDtypeStruct((m, (na + nb) * tn), _F32),
                   jax.ShapeDtypeStruct((m, LANES), _F32)],
        scratch_shapes=[pltpu.VMEM((tm, d), _BF16)],
        compiler_params=_params("arbitrary", "arbitrary"),
        name="in_proj",
    )(x, mod, mod, norm_g, wa, wb, wlr)


def _mixer_kernel(q_ref, k_ref, v_ref, r_ref, lr_ref, cb_ref, cc_ref, cx_ref, ga_ref, gb_ref,
                  wa2_ref, ba2_ref, gng_ref, cw_ref, wog_ref, woc_ref,
                  m_ref, state_ref, ucarry_ref, o_ref, og_ref, ubuf_ref, tb_ref, *, tps):
    tm = q_ref.shape[0]
    dk = q_ref.shape[1] // GLA_HEADS
    dv = v_ref.shape[1] // GLA_HEADS

    @pl.when(pl.program_id(0) % tps == 0)
    def _():
        state_ref[...] = jnp.zeros_like(state_ref)
        ucarry_ref[...] = jnp.zeros_like(ucarry_ref)

    lr1, lr2, _ = _split3(lr_ref[...])
    w1, w2, _ = _split3(wa2_ref[...])
    z = _dot(lr1, w1) + (_dot(lr2, w1) + _dot(lr1, w2)) + ba2_ref[...]
    la = (jnp.minimum(z, 0.0) - jnp.log1p(jnp.exp(-jnp.abs(z)))) / GLA_TAU

    row = lax.broadcasted_iota(jnp.int32, (tm, tm), 0)
    col = lax.broadcasted_iota(jnp.int32, (tm, tm), 1)
    upper = ((col > row) & ((col ^ row) < CHUNK)).astype(_BF16)
    a1, a2, a3 = _split3(la)
    rev = _dot(upper, a1) + (_dot(upper, a2) + _dot(upper, a3))
    k_dec = (k_ref[...] * jnp.exp(rev)).astype(_BF16)
    q = (q_ref[...] * (dk ** -0.5)).astype(_BF16)
    v = v_ref[...].astype(_BF16)

    chunks = [slice(c * CHUNK, (c + 1) * CHUNK) for c in range(tm // CHUNK)]
    heads = [(slice(h * dk, (h + 1) * dk), slice(h * dv, (h + 1) * dv)) for h in range(GLA_HEADS)]
    upd = [[lax.dot_general(v[rows, vs], k_dec[rows, ks], (((0,), (0,)), ((), ())),
                            preferred_element_type=_F32) for ks, vs in heads] for rows in chunks]
    st = [state_ref[h] for h in range(GLA_HEADS)]
    for c, rows in enumerate(chunks):
        decay_all = jnp.exp(rev[c * CHUNK:c * CHUNK + 1, :] + la[c * CHUNK:c * CHUNK + 1, :])
        for h, (ks, vs) in enumerate(heads):
            st[h] = st[h] * decay_all[:, ks] + upd[c][h]
            o_ref[rows, vs] = lax.dot_general(q[rows, ks], st[h].astype(_BF16),
                                              (((1,), (1,)), ((), ())),
                                              preferred_element_type=_F32)
    for h in range(GLA_HEADS):
        state_ref[h] = st[h]

    gng = gng_ref[...]
    for h in range(GLA_HEADS):
        vs = slice(h * dv, (h + 1) * dv)
        r = r_ref[:, vs]
        og_ref[:, vs] = (_rmsnorm(o_ref[:, vs], gng) * (r * _sigmoid(r))).astype(_BF16)
    y_a = _dot(og_ref[:, 0:GLA_HEADS * dv], wog_ref[...])

    cwid = cc_ref.shape[1]
    lanes = slice(0, cwid)
    ubuf_ref[0:SUBLANES, lanes] = ucarry_ref[...]
    ubuf_ref[SUBLANES:, lanes] = cc_ref[...] * cx_ref[...]
    ucarry_ref[...] = ubuf_ref[tm:, lanes]
    cw = cw_ref[...]
    for r in range(0, tm, MIX_ROWS):
        u = ubuf_ref[pl.ds(r, MIX_ROWS + SUBLANES), lanes]
        conv = (cw[0:1] * pltpu.roll(u, 2, 0)[SUBLANES:]
                + cw[1:2] * pltpu.roll(u, 1, 0)[SUBLANES:]
                + cw[2:3] * u[SUBLANES:])
        tb_ref[pl.ds(r, MIX_ROWS), lanes] = (cb_ref[pl.ds(r, MIX_ROWS), :] * conv).astype(_BF16)
    y_b = _dot(tb_ref[:, lanes], woc_ref[...])

    m = _sigmoid(ga_ref[...]) * y_a + _sigmoid(gb_ref[...]) * y_b
    m_ref[...] = m.astype(_BF16)


def _mixer(p, lr, wa2, ba2, gng, cw, wog, woc, layer, seq, tm=256):
    m = p.shape[0]
    qk = wa2.shape[2]
    gv, d = wog.shape[1:]
    cwid = woc.shape[1]
    dk, dv = qk // GLA_HEADS, gv // GLA_HEADS
    tps = seq // tm
    col = lambda width, idx: pl.BlockSpec((tm, width), lambda i: (i, idx))
    const = lambda *shape: pl.BlockSpec((None,) + shape, lambda i: (layer,) + (0,) * len(shape),
                                        pipeline_mode=pl.Buffered(1))
    assert gv == 2 * qk and cwid == gv and d == 2 * gv
    return pl.pallas_call(
        functools.partial(_mixer_kernel, tps=tps),
        grid=(m // tm,),
        in_specs=[
            col(qk, 0), col(qk, 1), col(gv, 1), col(gv, 2),
            pl.BlockSpec((tm, LANES), lambda i: (i, 0)),
            col(cwid, 3), col(cwid, 4), col(cwid, 5), col(d, 3), col(d, 4),
            const(LANES, qk), const(1, qk), const(1, dv), const(CONV_K, cwid),
            const(gv, d), const(cwid, d),
        ],
        out_specs=pl.BlockSpec((tm, d), lambda i: (i, 0)),
        out_shape=jax.ShapeDtypeStruct((m, d), _BF16),
        scratch_shapes=[
            pltpu.VMEM((GLA_HEADS, dv, dk), _F32),
            pltpu.VMEM((SUBLANES, cwid), _F32),
            pltpu.VMEM((tm, gv + LANES), _F32),
            pltpu.VMEM((tm, gv + LANES), _BF16),
            pltpu.VMEM((tm + SUBLANES, cwid + LANES), _F32),
            pltpu.VMEM((tm, cwid + LANES), _BF16),
        ],
        compiler_params=_params("arbitrary"),
        name="token_mixer",
    )(p, p, p, p, lr, p, p, p, p, p, wa2, ba2, gng, cw, wog, woc)


def _oproj_kernel(m_ref, x_ref, w_ref, g_ref, gate_ref, o_ref):
    y = _dot(m_ref[...], w_ref[...])
    o_ref[...] = x_ref[...] + gate_ref[...] * _rmsnorm(y, g_ref[...])


def _oproj(mm, x, w, norm_g, mod, layer, seq, tm=512):
    m, d = x.shape
    tps = seq // tm
    return pl.pallas_call(
        _oproj_kernel,
        grid=(m // tm,),
        in_specs=[
            pl.BlockSpec((tm, d), lambda i: (i, 0)),
            pl.BlockSpec((tm, d), lambda i: (i, 0)),
            pl.BlockSpec((None, d, d), lambda i: (layer, 0, 0), pipeline_mode=pl.Buffered(1)),
            pl.BlockSpec((None, None, 1, d), lambda i: (layer, 1, 0, 0)),
            pl.BlockSpec((None, None, None, 1, d), lambda i: (layer, i // tps, 2, 0, 0)),
        ],
        out_specs=pl.BlockSpec((tm, d), lambda i: (i, 0)),
        out_shape=jax.ShapeDtypeStruct((m, d), _F32),
        compiler_params=_params("arbitrary"),
        name="out_proj",
    )(mm, x, w, norm_g, mod)


def _gelu_tanh(x):
    return 0.5 * x * (1.0 + jnp.tanh(0.7978845608028654 * (x + 0.044715 * (x * x * x))))


def _ffn_kernel(x_ref, sc_ref, sh_ref, gate_ref, gin_ref, gout_ref, wg_ref, wu_ref, cw_ref, wd_ref,
                o_ref, h_ref, z_ref, g0_ref, g1_ref, u0_ref, u1_ref, carry_ref, *, tps, nj):
    i, s = pl.program_id(0), pl.program_id(1)
    tm, d = x_ref.shape
    tn = wg_ref.shape[1]
    dff, tnd = wd_ref.shape
    gbufs, ubufs = (g0_ref, g1_ref), (u0_ref, u1_ref)
    lanes = slice(0, tn)

    def up_project(slot):
        h = h_ref[:, 0:d]
        gbufs[slot][SUBLANES:, lanes] = _dot(h, wg_ref[...])
        ubufs[slot][:, lanes] = _dot(h, wu_ref[...])

    def gate(slot, j):
        gbuf = gbufs[slot]

        gbuf[0:SUBLANES, lanes] = carry_ref[j]
        carry_ref[j] = gbuf[tm:, lanes]
        cw = cw_ref[...]
        cols = (pl.ds(j * tn, tn) if isinstance(j, int)
                else pl.ds(pl.multiple_of(j * tn, tn), tn))
        for r in range(0, tm, GATE_ROWS):
            g = gbuf[pl.ds(r, GATE_ROWS + SUBLANES), lanes]
            conv = (cw[0:1] * pltpu.roll(g, 2, 0)[SUBLANES:]
                    + cw[1:2] * pltpu.roll(g, 1, 0)[SUBLANES:]
                    + cw[2:3] * g[SUBLANES:])
            z = _gelu_tanh(conv) * ubufs[slot][pl.ds(r, GATE_ROWS), lanes]
            z_ref[pl.ds(r, GATE_ROWS), cols] = z.astype(_BF16)

    @pl.when((s == 0) & (i % tps == 0))
    def _():
        carry_ref[...] = jnp.zeros_like(carry_ref)

    @pl.when(s == 0)
    def _():
        h = _rmsnorm(x_ref[...], gin_ref[...]) * (1.0 + sc_ref[...]) + sh_ref[...]
        h_ref[:, 0:d] = h.astype(_BF16)
        up_project(0)

    for parity in (0, 1):
        @pl.when((s >= 1) & (s < nj) & (s % 2 == parity))
        def _():
            gate(1 - parity, s - 1)
            up_project(parity)

    @pl.when(s == nj)
    def _():
        gate((nj - 1) % 2, nj - 1)

    @pl.when(s >= nj)
    def _():
        n = s - nj
        o_ref[:, pl.ds(pl.multiple_of(n * tnd, tnd), tnd)] = _dot(z_ref[:, 0:dff], wd_ref[...])

    @pl.when(s == pl.num_programs(1) - 1)
    def _():
        o_ref[...] = x_ref[...] + gate_ref[...] * _rmsnorm(o_ref[...], gout_ref[...])


def _ffn(x, mod, norm_g, w_up, cw, w_down, layer, seq, tm=512, tn=FFN_TN, tnd=FFN_TND):
    m, d = x.shape
    dff = w_down.shape[1]
    nj, nn = dff // tn, d // tnd
    tps = seq // tm
    pad = LANES
    mod_spec = lambda idx: pl.BlockSpec((None, None, None, 1, d),
                                        lambda i, s: (layer, i // tps, idx, 0, 0))
    vec = lambda idx: pl.BlockSpec((None, None, 1, d), lambda i, s: (layer, idx, 0, 0))
    return pl.pallas_call(
        functools.partial(_ffn_kernel, tps=tps, nj=nj),
        grid=(m // tm, nj + nn),
        in_specs=[
            pl.BlockSpec((tm, d), lambda i, s: (i, 0)),
            mod_spec(4), mod_spec(3), mod_spec(5), vec(2), vec(3),
            pl.BlockSpec((None, d, tn), lambda i, s: (layer, 0, jnp.minimum(s, nj - 1))),
            pl.BlockSpec((None, d, tn), lambda i, s: (layer, 0, jnp.minimum(s, nj - 1) + nj)),
            pl.BlockSpec((None, CONV_K, tn), lambda i, s: (layer, 0, jnp.clip(s - 1, 0, nj - 1))),
            pl.BlockSpec((None, dff, tnd), lambda i, s: (layer, 0, jnp.maximum(s - nj, 0))),
        ],
        out_specs=pl.BlockSpec((tm, d), lambda i, s: (i, 0)),
        out_shape=jax.ShapeDtypeStruct((m, d), _F32),
        scratch_shapes=[
            pltpu.VMEM((tm, d + pad), _BF16),
            pltpu.VMEM((tm, dff + pad), _BF16),
            pltpu.VMEM((tm + SUBLANES, tn + pad), _F32),
            pltpu.VMEM((tm + SUBLANES, tn + pad), _F32),
            pltpu.VMEM((tm, tn + pad), _F32),
            pltpu.VMEM((tm, tn + pad), _F32),
            pltpu.VMEM((nj, SUBLANES, tn), _F32),
        ],
        compiler_params=_params("arbitrary", "arbitrary"),
        name="conv_ffn",
    )(x, mod, mod, mod, norm_g, norm_g, w_up, w_up, cw, w_down)


def kernel(x, c, w_ada, b_ada, norm_g, w_in, w_a2, b_a2, gla_norm_g, w_out_gla, conv_mix_w,
           w_out_conv, w_o, w_up, ffn_conv_w, w_down):
    bsz, seq, d = x.shape
    depth = w_ada.shape[0]
    qk = w_a2.shape[2]
    gv = w_out_gla.shape[1]
    lr0 = 2 * qk + 2 * gv

    mod = _modulation(c, w_ada, b_ada)
    w_in_a = w_in[:, :, :lr0].astype(_BF16)
    w_in_b = w_in[:, :, lr0 + GLA_LOWRANK:].astype(_BF16)
    w_lr = jnp.pad(w_in[:, :, lr0:lr0 + GLA_LOWRANK].astype(_BF16),
                   ((0, 0), (0, 0), (0, LANES - GLA_LOWRANK)))
    wa2 = jnp.pad(w_a2, ((0, 0), (0, LANES - GLA_LOWRANK), (0, 0)))
    w_og, w_oc, w_ob = w_out_gla.astype(_BF16), w_out_conv.astype(_BF16), w_o.astype(_BF16)
    w_upb, w_downb = w_up.astype(_BF16), w_down.astype(_BF16)
    ng = norm_g.reshape(depth, 4, 1, d)
    ba2 = b_a2.reshape(depth, 1, qk)
    gng = gla_norm_g.reshape(depth, 1, -1)

    xf = x.reshape(bsz * seq, d)
    for l in range(depth):
        p, lr = _proj(xf, mod, ng, w_in_a, w_in_b, w_lr, l, seq)
        mm = _mixer(p, lr, wa2, ba2, gng, conv_mix_w, w_og, w_oc, l, seq)
        xf = _oproj(mm, xf, w_ob, ng, mod, l, seq)
        xf = _ffn(xf, mod, ng, w_upb, ffn_conv_w, w_downb, l, seq)
    return xf.reshape(bsz, seq, d)
```

```python
import functools

import jax
import jax.numpy as jnp
from jax import lax
from jax.experimental import pallas as pl
from jax.experimental.pallas import tpu as pltpu

CHUNK = 64
GLA_HEADS = 4
GLA_LOWRANK = 16
GLA_TAU = 16.0
CONV_K = 3
EPS = 1e-6

LANES = 128
SUBLANES = 8
GATE_ROWS = 32
MIX_ROWS = 32
PROJ_TN = 1024
FFN_TN = 512
FFN_TND = 512
VMEM_LIMIT_BYTES = 56 * 1024 * 1024

_F32 = jnp.float32
_BF16 = jnp.bfloat16


def _rmsnorm(x, g):
    return x * lax.rsqrt(jnp.mean(x * x, axis=-1, keepdims=True) + EPS) * g


def _sigmoid(x):
    return 0.5 * jnp.tanh(0.5 * x) + 0.5


def _dot(a, b):
    return jnp.dot(a, b, preferred_element_type=_F32)


def _split3(a):
    a1 = a.astype(_BF16)
    r1 = a - a1.astype(_F32)
    a2 = r1.astype(_BF16)
    a3 = (r1 - a2.astype(_F32)).astype(_BF16)
    return a1, a2, a3


def _params(*semantics):
    return pltpu.CompilerParams(dimension_semantics=semantics, vmem_limit_bytes=VMEM_LIMIT_BYTES)


def _mod_kernel(c_ref, w_ref, b_ref, o_ref):
    c = c_ref[...]
    s = (c * jax.nn.sigmoid(c)).astype(_BF16)
    o_ref[...] = _dot(s, w_ref[...].astype(_BF16)) + b_ref[...]


def _modulation(c, w_ada, b_ada, tn=1024):
    depth, d, n = w_ada.shape
    b = c.shape[0]
    rows = -(-b // SUBLANES) * SUBLANES
    c_pad = jnp.pad(c, ((0, rows - b), (0, 0)))
    out = pl.pallas_call(
        _mod_kernel,
        grid=(depth, n // tn),
        in_specs=[
            pl.BlockSpec((rows, d), lambda l, j: (0, 0)),
            pl.BlockSpec((None, d, tn), lambda l, j: (l, 0, j)),
            pl.BlockSpec((None, 1, tn), lambda l, j: (l, 0, j)),
        ],
        out_specs=pl.BlockSpec((None, rows, tn), lambda l, j: (l, 0, j)),
        out_shape=jax.ShapeDtypeStruct((depth, rows, n), _F32),
        compiler_params=_params("arbitrary", "arbitrary"),
        name="adaln_mod",
    )(c_pad, w_ada, b_ada.reshape(depth, 1, n))
    return out[:, :b].reshape(depth, b, 6, 1, d)


def _proj_kernel(x_ref, sc_ref, sh_ref, g_ref, wa_ref, wb_ref, wlr_ref, o_ref, lr_ref, h_ref, *, na):
    j = pl.program_id(1)

    @pl.when(j == 0)
    def _():
        h = _rmsnorm(x_ref[...], g_ref[...]) * (1.0 + sc_ref[...]) + sh_ref[...]
        h_ref[...] = h.astype(_BF16)
        lr_ref[...] = _dot(h_ref[...], wlr_ref[...])

    @pl.when(j < na)
    def _():
        o_ref[...] = _dot(h_ref[...], wa_ref[...])

    @pl.when(j >= na)
    def _():
        o_ref[...] = _dot(h_ref[...], wb_ref[...])


def _proj(x, mod, norm_g, wa, wb, wlr, layer, seq, tm=1024, tn=PROJ_TN):
    m, d = x.shape
    na, nb = wa.shape[2] // tn, wb.shape[2] // tn
    tps = seq // tm
    mod_spec = lambda idx: pl.BlockSpec((None, None, None, 1, d),
                                        lambda i, j: (layer, i // tps, idx, 0, 0))
    return pl.pallas_call(
        functools.partial(_proj_kernel, na=na),
        grid=(m // tm, na + nb),
        in_specs=[
            pl.BlockSpec((tm, d), lambda i, j: (i, 0)),
            mod_spec(1), mod_spec(0),
            pl.BlockSpec((None, None, 1, d), lambda i, j: (layer, 0, 0, 0)),
            pl.BlockSpec((None, d, tn), lambda i, j: (layer, 0, jnp.minimum(j, na - 1))),
            pl.BlockSpec((None, d, tn), lambda i, j: (layer, 0, jnp.maximum(j - na, 0))),
            pl.BlockSpec((None, d, LANES), lambda i, j: (layer, 0, 0)),
        ],
        out_specs=[
            pl.BlockSpec((tm, tn), lambda i, j: (i, j)),
            pl.BlockSpec((tm, LANES), lambda i, j: (i, 0)),
        ],
        out_shape=[jax.ShapeDtypeStruct((m, (na + nb) * tn), _F32),
                   jax.ShapeDtypeStruct((m, LANES), _F32)],
        scratch_shapes=[pltpu.VMEM((tm, d), _BF16)],
        compiler_params=_params("arbitrary", "arbitrary"),
        name="in_proj",
    )(x, mod, mod, norm_g, wa, wb, wlr)


def _mixer_kernel(q_ref, k_ref, v_ref, r_ref, lr_ref, cb_ref, cc_ref, cx_ref, ga_ref, gb_ref,
                  wa2_ref, ba2_ref, gng_ref, cw_ref, wog_ref, woc_ref,
                  x_ref, wo_ref, ng_ref, xgate_ref,
                  xo_ref, state_ref, ucarry_ref, o_ref, og_ref, ubuf_ref, tb_ref, *, tps):
    tm = q_ref.shape[0]
    dk = q_ref.shape[1] // GLA_HEADS
    dv = v_ref.shape[1] // GLA_HEADS

    @pl.when(pl.program_id(0) % tps == 0)
    def _():
        state_ref[...] = jnp.zeros_like(state_ref)
        ucarry_ref[...] = jnp.zeros_like(ucarry_ref)

    lr1, lr2, _ = _split3(lr_ref[...])
    w1, w2, _ = _split3(wa2_ref[...])
    z = _dot(lr1, w1) + (_dot(lr2, w1) + _dot(lr1, w2)) + ba2_ref[...]
    la = (jnp.minimum(z, 0.0) - jnp.log1p(jnp.exp(-jnp.abs(z)))) / GLA_TAU

    row = lax.broadcasted_iota(jnp.int32, (tm, tm), 0)
    col = lax.broadcasted_iota(jnp.int32, (tm, tm), 1)
    upper = ((col > row) & ((col ^ row) < CHUNK)).astype(_BF16)
    a1, a2, a3 = _split3(la)
    rev = _dot(upper, a1) + (_dot(upper, a2) + _dot(upper, a3))
    k_dec = (k_ref[...] * jnp.exp(rev)).astype(_BF16)
    q = (q_ref[...] * (dk ** -0.5)).astype(_BF16)
    v = v_ref[...].astype(_BF16)

    chunks = [slice(c * CHUNK, (c + 1) * CHUNK) for c in range(tm // CHUNK)]
    heads = [(slice(h * dk, (h + 1) * dk), slice(h * dv, (h + 1) * dv)) for h in range(GLA_HEADS)]
    upd = [[lax.dot_general(v[rows, vs], k_dec[rows, ks], (((0,), (0,)), ((), ())),
                            preferred_element_type=_F32) for ks, vs in heads] for rows in chunks]
    st = [state_ref[h] for h in range(GLA_HEADS)]
    for c, rows in enumerate(chunks):
        decay_all = jnp.exp(rev[c * CHUNK:c * CHUNK + 1, :] + la[c * CHUNK:c * CHUNK + 1, :])
        for h, (ks, vs) in enumerate(heads):
            st[h] = st[h] * decay_all[:, ks] + upd[c][h]
            o_ref[rows, vs] = lax.dot_general(q[rows, ks], st[h].astype(_BF16),
                                              (((1,), (1,)), ((), ())),
                                              preferred_element_type=_F32)
    for h in range(GLA_HEADS):
        state_ref[h] = st[h]

    gng = gng_ref[...]
    for h in range(GLA_HEADS):
        vs = slice(h * dv, (h + 1) * dv)
        r = r_ref[:, vs]
        og_ref[:, vs] = (_rmsnorm(o_ref[:, vs], gng) * (r * _sigmoid(r))).astype(_BF16)
    y_a = _dot(og_ref[:, 0:GLA_HEADS * dv], wog_ref[...])

    cwid = cc_ref.shape[1]
    lanes = slice(0, cwid)
    ubuf_ref[0:SUBLANES, lanes] = ucarry_ref[...]
    ubuf_ref[SUBLANES:, lanes] = cc_ref[...] * cx_ref[...]
    ucarry_ref[...] = ubuf_ref[tm:, lanes]
    cw = cw_ref[...]
    for r in range(0, tm, MIX_ROWS):
        u = ubuf_ref[pl.ds(r, MIX_ROWS + SUBLANES), lanes]
        conv = (cw[0:1] * pltpu.roll(u, 2, 0)[SUBLANES:]
                + cw[1:2] * pltpu.roll(u, 1, 0)[SUBLANES:]
                + cw[2:3] * u[SUBLANES:])
        tb_ref[pl.ds(r, MIX_ROWS), lanes] = (cb_ref[pl.ds(r, MIX_ROWS), :] * conv).astype(_BF16)
    y_b = _dot(tb_ref[:, lanes], woc_ref[...])

    m = _sigmoid(ga_ref[...]) * y_a + _sigmoid(gb_ref[...]) * y_b
    y = _dot(m.astype(_BF16), wo_ref[...])
    xo_ref[...] = x_ref[...] + xgate_ref[...] * _rmsnorm(y, ng_ref[...])


def _mixer(p, lr, wa2, ba2, gng, cw, wog, woc, x, wo, norm_g, mod, layer, seq, tm=256):
    m = p.shape[0]
    qk = wa2.shape[2]
    gv, d = wog.shape[1:]
    cwid = woc.shape[1]
    dk, dv = qk // GLA_HEADS, gv // GLA_HEADS
    tps = seq // tm
    col = lambda width, idx: pl.BlockSpec((tm, width), lambda i: (i, idx))
    const = lambda *shape: pl.BlockSpec((None,) + shape, lambda i: (layer,) + (0,) * len(shape),
                                        pipeline_mode=pl.Buffered(1))
    assert gv == 2 * qk and cwid == gv and d == 2 * gv
    return pl.pallas_call(
        functools.partial(_mixer_kernel, tps=tps),
        grid=(m // tm,),
        in_specs=[
            col(qk, 0), col(qk, 1), col(gv, 1), col(gv, 2),
            pl.BlockSpec((tm, LANES), lambda i: (i, 0)),
            col(cwid, 3), col(cwid, 4), col(cwid, 5), col(d, 3), col(d, 4),
            const(LANES, qk), const(1, qk), const(1, dv), const(CONV_K, cwid),
            const(gv, d), const(cwid, d),
            pl.BlockSpec((tm, d), lambda i: (i, 0)),
            const(d, d),
            pl.BlockSpec((None, None, 1, d), lambda i: (layer, 1, 0, 0)),
            pl.BlockSpec((None, None, None, 1, d), lambda i: (layer, i // tps, 2, 0, 0)),
        ],
        out_specs=pl.BlockSpec((tm, d), lambda i: (i, 0)),
        out_shape=jax.ShapeDtypeStruct((m, d), _F32),
        scratch_shapes=[
            pltpu.VMEM((GLA_HEADS, dv, dk), _F32),
            pltpu.VMEM((SUBLANES, cwid), _F32),
            pltpu.VMEM((tm, gv + LANES), _F32),
            pltpu.VMEM((tm, gv + LANES), _BF16),
            pltpu.VMEM((tm + SUBLANES, cwid + LANES), _F32),
            pltpu.VMEM((tm, cwid + LANES), _BF16),
        ],
        compiler_params=_params("arbitrary"),
        name="token_mixer",
    )(p, p, p, p, lr, p, p, p, p, p, wa2, ba2, gng, cw, wog, woc, x, wo, norm_g, mod)


def _gelu_tanh(x):
    return 0.5 * x * (1.0 + jnp.tanh(0.7978845608028654 * (x + 0.044715 * (x * x * x))))


def _ffn_kernel(x_ref, sc_ref, sh_ref, gate_ref, gin_ref, gout_ref, wg_ref, wu_ref, cw_ref, wd_ref,
                o_ref, h_ref, z_ref, g0_ref, g1_ref, u0_ref, u1_ref, carry_ref, *, tps, nj):
    i, s = pl.program_id(0), pl.program_id(1)
    tm, d = x_ref.shape
    tn = wg_ref.shape[1]
    dff, tnd = wd_ref.shape
    gbufs, ubufs = (g0_ref, g1_ref), (u0_ref, u1_ref)
    lanes = slice(0, tn)

    def up_project(slot):
        h = h_ref[:, 0:d]
        gbufs[slot][SUBLANES:, lanes] = _dot(h, wg_ref[...])
        ubufs[slot][:, lanes] = _dot(h, wu_ref[...])

    def gate(slot, j):
        gbuf = gbufs[slot]

        gbuf[0:SUBLANES, lanes] = carry_ref[j]
        carry_ref[j] = gbuf[tm:, lanes]
        cw = cw_ref[...]
        cols = (pl.ds(j * tn, tn) if isinstance(j, int)
                else pl.ds(pl.multiple_of(j * tn, tn), tn))
        for r in range(0, tm, GATE_ROWS):
            g = gbuf[pl.ds(r, GATE_ROWS + SUBLANES), lanes]
            conv = (cw[0:1] * pltpu.roll(g, 2, 0)[SUBLANES:]
                    + cw[1:2] * pltpu.roll(g, 1, 0)[SUBLANES:]
                    + cw[2:3] * g[SUBLANES:])
            z = _gelu_tanh(conv) * ubufs[slot][pl.ds(r, GATE_ROWS), lanes]
            z_ref[pl.ds(r, GATE_ROWS), cols] = z.astype(_BF16)

    @pl.when((s == 0) & (i % tps == 0))
    def _():
        carry_ref[...] = jnp.zeros_like(carry_ref)

    @pl.when(s == 0)
    def _():
        h = _rmsnorm(x_ref[...], gin_ref[...]) * (1.0 + sc_ref[...]) + sh_ref[...]
        h_ref[:, 0:d] = h.astype(_BF16)
        up_project(0)

    for parity in (0, 1):
        @pl.when((s >= 1) & (s < nj) & (s % 2 == parity))
        def _():
            gate(1 - parity, s - 1)
            up_project(parity)

    @pl.when(s == nj)
    def _():
        gate((nj - 1) % 2, nj - 1)

    @pl.when(s >= nj)
    def _():
        n = s - nj
        o_ref[:, pl.ds(pl.multiple_of(n * tnd, tnd), tnd)] = _dot(z_ref[:, 0:dff], wd_ref[...])

    @pl.when(s == pl.num_programs(1) - 1)
    def _():
        o_ref[...] = x_ref[...] + gate_ref[...] * _rmsnorm(o_ref[...], gout_ref[...])


def _ffn(x, mod, norm_g, w_up, cw, w_down, layer, seq, tm=512, tn=FFN_TN, tnd=FFN_TND):
    m, d = x.shape
    dff = w_down.shape[1]
    nj, nn = dff // tn, d // tnd
    tps = seq // tm
    pad = LANES
    mod_spec = lambda idx: pl.BlockSpec((None, None, None, 1, d),
                                        lambda i, s: (layer, i // tps, idx, 0, 0))
    vec = lambda idx: pl.BlockSpec((None, None, 1, d), lambda i, s: (layer, idx, 0, 0))
    return pl.pallas_call(
        functools.partial(_ffn_kernel, tps=tps, nj=nj),
        grid=(m // tm, nj + nn),
        in_specs=[
            pl.BlockSpec((tm, d), lambda i, s: (i, 0)),
            mod_spec(4), mod_spec(3), mod_spec(5), vec(2), vec(3),
            pl.BlockSpec((None, d, tn), lambda i, s: (layer, 0, jnp.minimum(s, nj - 1))),
            pl.BlockSpec((None, d, tn), lambda i, s: (layer, 0, jnp.minimum(s, nj - 1) + nj)),
            pl.BlockSpec((None, CONV_K, tn), lambda i, s: (layer, 0, jnp.clip(s - 1, 0, nj - 1))),
            pl.BlockSpec((None, dff, tnd), lambda i, s: (layer, 0, jnp.maximum(s - nj, 0))),
        ],
        out_specs=pl.BlockSpec((tm, d), lambda i, s: (i, 0)),
        out_shape=jax.ShapeDtypeStruct((m, d), _F32),
        scratch_shapes=[
            pltpu.VMEM((tm, d + pad), _BF16),
            pltpu.VMEM((tm, dff + pad), _BF16),
            pltpu.VMEM((tm + SUBLANES, tn + pad), _F32),
            pltpu.VMEM((tm + SUBLANES, tn + pad), _F32),
            pltpu.VMEM((tm, tn + pad), _F32),
            pltpu.VMEM((tm, tn + pad), _F32),
            pltpu.VMEM((nj, SUBLANES, tn), _F32),
        ],
        compiler_params=_params("arbitrary", "arbitrary"),
        name="conv_ffn",
    )(x, mod, mod, mod, norm_g, norm_g, w_up, w_up, cw, w_down)


def kernel(x, c, w_ada, b_ada, norm_g, w_in, w_a2, b_a2, gla_norm_g, w_out_gla, conv_mix_w,
           w_out_conv, w_o, w_up, ffn_conv_w, w_down):
    bsz, seq, d = x.shape
    depth = w_ada.shape[0]
    qk = w_a2.shape[2]
    gv = w_out_gla.shape[1]
    lr0 = 2 * qk + 2 * gv

    mod = _modulation(c, w_ada, b_ada)
    w_in_a = w_in[:, :, :lr0].astype(_BF16)
    w_in_b = w_in[:, :, lr0 + GLA_LOWRANK:].astype(_BF16)
    w_lr = jnp.pad(w_in[:, :, lr0:lr0 + GLA_LOWRANK].astype(_BF16),
                   ((0, 0), (0, 0), (0, LANES - GLA_LOWRANK)))
    wa2 = jnp.pad(w_a2, ((0, 0), (0, LANES - GLA_LOWRANK), (0, 0)))
    w_og, w_oc, w_ob = w_out_gla.astype(_BF16), w_out_conv.astype(_BF16), w_o.astype(_BF16)
    w_upb, w_downb = w_up.astype(_BF16), w_down.astype(_BF16)
    ng = norm_g.reshape(depth, 4, 1, d)
    ba2 = b_a2.reshape(depth, 1, qk)
    gng = gla_norm_g.reshape(depth, 1, -1)

    xf = x.reshape(bsz * seq, d)
    for l in range(depth):
        p, lr = _proj(xf, mod, ng, w_in_a, w_in_b, w_lr, l, seq)
        xf = _mixer(p, lr, wa2, ba2, gng, conv_mix_w, w_og, w_oc, xf, w_ob, ng, mod, l, seq)
        xf = _ffn(xf, mod, ng, w_upb, ffn_conv_w, w_downb, l, seq)
    return xf.reshape(bsz, seq, d)
```
